```python
import math
import jax, jax.numpy as jnp
from jax import lax
import numpy as np

D_MODEL = 1024
BATCH = 2
SEQ = 16384
DEPTH = 1
DEC_BATCH = 16
DEC_SEQ = 16
PAST_LEN = 1024

CHUNK = 64
ATT_HEADS = 4
QK_DIM = 64
V_DIM = 2 * QK_DIM
ATT_WIDTH = ATT_HEADS * V_DIM
ROT_DIM = QK_DIM // 4
ROPE_THETA = 500000.0
Q_BLOCK = 128
LRU_WIDTH = 512
LRU_BLOCKS = 8
LRU_BLOCK_DIM = LRU_WIDTH // LRU_BLOCKS
CONV_WIDTH = 4
LRU_C = 8.0
MIX_WIDTH = ATT_WIDTH + LRU_WIDTH
Q_COLS = ATT_HEADS * 2 * QK_DIM
K_COLS = ATT_HEADS * 2 * QK_DIM
V_COLS = ATT_HEADS * V_DIM
IN_COLS = Q_COLS + K_COLS + V_COLS + 2 * LRU_WIDTH
PEER_HEADS = 8
PEER_NKEYS = 128
PEER_EXPERTS = PEER_NKEYS * PEER_NKEYS
PEER_DKEY = 256
PEER_HALF = PEER_DKEY // 2
PEER_TOPK = 16
PEER_BLOCK = 128
PLE_DIM = 256
EPS = 1e-6

kernel_name = "hybrid_diffattn_rglru_peer_stream_step"


def _rmsnorm(x, g):
    xf = x.astype(jnp.float32)
    ms = jnp.mean(xf * xf, axis=-1, keepdims=True)
    return (xf * lax.rsqrt(ms + EPS) * g.astype(jnp.float32)).astype(x.dtype)


def _rope(x, pos):
    half = ROT_DIM // 2
    inv = ROPE_THETA ** (-jnp.arange(0, ROT_DIM, 2, dtype=jnp.float32) / ROT_DIM)
    ang = pos.astype(jnp.float32)[:, None] * inv[None, :]
    cos = jnp.cos(ang)[:, None, None, :]
    sin = jnp.sin(ang)[:, None, None, :]
    xr = x[..., :ROT_DIM].astype(jnp.float32)
    x1, x2 = xr[..., :half], xr[..., half:]
    rot = jnp.concatenate([x1 * cos - x2 * sin, x2 * cos + x1 * sin], axis=-1).astype(x.dtype)
    return jnp.concatenate([rot, x[..., ROT_DIM:]], axis=-1)


def _diff_attn(q, k, v, q_pos, k_pos, lam):
    s = jnp.einsum('bqhmd,bkhmd->bhmqk', q, k).astype(jnp.float32) * (QK_DIM ** -0.5)
    mask = (k_pos[None, :] // CHUNK) <= (q_pos[:, None] // CHUNK)
    s = jnp.where(mask, s, -jnp.inf)
    p = jax.nn.softmax(s, axis=-1)
    w = p[:, :, 0] - lam * p[:, :, 1]
    return jnp.einsum('bhqk,bkhd->bqhd', w.astype(v.dtype), v)


def _diff_attn_prompt(q, k, v, lam):
    B, S = q.shape[0], q.shape[1]
    nb = S // Q_BLOCK
    qb = q.reshape(B, nb, Q_BLOCK, ATT_HEADS, 2, QK_DIM).swapaxes(0, 1)
    k_pos = jnp.arange(S)

    def body(args):
        qi, bi = args
        return _diff_attn(qi, k, v, bi * Q_BLOCK + jnp.arange(Q_BLOCK), k_pos, lam)

    o = lax.map(body, (qb, jnp.arange(nb)))
    return o.swapaxes(0, 1).reshape(B, S, ATT_HEADS, V_DIM)


def _rglru_branch(xb, gb, conv_buf, h0, conv_w, conv_b, w_a, b_a, w_x, b_x, lru_lambda):
    T = xb.shape[1]
    xp = jnp.concatenate([conv_buf.astype(xb.dtype), xb], axis=1)
    xc = sum((xp[:, j:j + T] * conv_w[j] for j in range(CONV_WIDTH)), conv_b)
    new_buf = xp[:, -(CONV_WIDTH - 1):]
    xcb = xc.reshape(xc.shape[0], T, LRU_BLOCKS, LRU_BLOCK_DIM)
    r = jax.nn.sigmoid(jnp.einsum('btni,nij->btnj', xcb, w_a).reshape(xc.shape) + b_a)
    i = jax.nn.sigmoid(jnp.einsum('btni,nij->btnj', xcb, w_x).reshape(xc.shape) + b_x)
    log_a = -LRU_C * jax.nn.softplus(-lru_lambda.astype(jnp.float32)) * r.astype(jnp.float32)
    a = jnp.exp(log_a)
    b = jnp.sqrt(-jnp.expm1(2.0 * log_a)) * (i * xc).astype(jnp.float32)
    b = b.at[:, 0].add(a[:, 0] * h0.astype(jnp.float32))

    def comb(left, right):
        return (left[0] * right[0], right[0] * left[1] + right[1])

    _, h = lax.associative_scan(comb, (a, b), axis=1)
    out = h.astype(xb.dtype) * jax.nn.gelu(gb)
    return out, new_buf, h[:, -1].astype(xb.dtype)


def _peer(c, w_pq, sub_keys, expert_u, expert_v):
    lead = c.shape[:-1]
    c2 = c.reshape(-1, D_MODEL)
    T = c2.shape[0]
    pad = (-T) % PEER_BLOCK
    blocks = jnp.pad(c2, ((0, pad), (0, 0))).reshape(-1, PEER_BLOCK, D_MODEL)

    def block_fn(cb):
        P = cb.shape[0]
        q = (cb @ w_pq).reshape(P, PEER_HEADS, 2, PEER_HALF)
        s = jnp.einsum('phsd,hskd->phsk', q, sub_keys).astype(jnp.float32)
        sv, si = lax.top_k(s, PEER_TOPK)
        cand = (sv[:, :, 0, :, None] + sv[:, :, 1, None, :]).reshape(P, PEER_HEADS, PEER_TOPK * PEER_TOPK)
        cidx = (si[:, :, 0, :, None] * PEER_NKEYS + si[:, :, 1, None, :]).reshape(P, PEER_HEADS, PEER_TOPK * PEER_TOPK)
        tv, tp = lax.top_k(cand, PEER_TOPK)
        eidx = jnp.take_along_axis(cidx, tp, axis=-1)
        g = jax.nn.softmax(tv, axis=-1)
        hact = jnp.einsum('phkd,pd->phk', expert_u[eidx], cb).astype(jnp.float32)
        wgt = (jax.nn.gelu(hact) * g).astype(cb.dtype)
        return jnp.einsum('phk,phkd->pd', wgt, expert_v[eidx])

    out = lax.map(block_fn, blocks).reshape(-1, D_MODEL)[:T]
    return out.reshape(*lead, D_MODEL)


def _layer(h, p_l, q_pos, past_k, past_v, conv_buf, h0, lam_init, w):
    B, T = h.shape[0], h.shape[1]
    a = _rmsnorm(h, w['g_mix'])
    z = a @ w['w_in']
    o1 = Q_COLS
    o2 = o1 + K_COLS
    o3 = o2 + V_COLS
    o4 = o3 + LRU_WIDTH
    q = _rope(z[..., :o1].reshape(B, T, ATT_HEADS, 2, QK_DIM), q_pos)
    k = _rope(z[..., o1:o2].reshape(B, T, ATT_HEADS, 2, QK_DIM), q_pos)
    v = z[..., o2:o3].reshape(B, T, ATT_HEADS, V_DIM)
    xb = z[..., o3:o4]
    gb = z[..., o4:]
    lam = (jnp.exp(jnp.sum(w['lam_q1'].astype(jnp.float32) * w['lam_k1'].astype(jnp.float32)))
           - jnp.exp(jnp.sum(w['lam_q2'].astype(jnp.float32) * w['lam_k2'].astype(jnp.float32)))
           + lam_init)
    if past_k is None:
        o = _diff_attn_prompt(q, k, v, lam)
    else:
        kf = jnp.concatenate([past_k, k], axis=1)
        vf = jnp.concatenate([past_v, v], axis=1)
        o = _diff_attn(q, kf, vf, q_pos, jnp.arange(kf.shape[1]), lam)
    o = (_rmsnorm(o, w['g_subln']) * (1.0 - lam_init)).reshape(B, T, ATT_WIDTH)
    r_out, new_buf, h_last = _rglru_branch(xb, gb, conv_buf, h0, w['conv_w'], w['conv_b'],
                                           w['w_a'], w['b_a'], w['w_x'], w['b_x'], w['lru_lambda'])
    h = h + jnp.concatenate([o, r_out], axis=-1) @ w['w_out']
    h = h + _peer(_rmsnorm(h, w['g_ffn']), w['w_pq'], w['sub_keys'], w['expert_u'], w['expert_v'])
    h = h + jax.nn.sigmoid(_rmsnorm(h, w['g_ple']) @ w['w_pg']) * (p_l @ w['w_pe'])
    return h, k, v, new_buf, h_last


def setup_inputs(seed: int = 0) -> dict:
    key = jax.random.key(seed)
    ks = jax.random.split(key, 40)
    nrm = lambda i, shape, s=1.0: jax.random.normal(ks[i], shape, jnp.float32) * s
    a8 = jax.random.uniform(ks[39], (DEPTH, LRU_WIDTH), jnp.float32, 0.9, 0.999)
    sig = a8 ** (1.0 / LRU_C)
    return {
        "x_prompt": nrm(0, (BATCH, SEQ, D_MODEL)),
        "x_sample": nrm(1, (DEC_BATCH, DEC_SEQ, D_MODEL)),
        "p_prompt": nrm(2, (DEPTH, BATCH, SEQ, PLE_DIM)),
        "p_sample": nrm(3, (DEPTH, DEC_BATCH, DEC_SEQ, PLE_DIM)),
        "cache_k": nrm(4, (DEPTH, DEC_BATCH, PAST_LEN, ATT_HEADS, 2, QK_DIM)),
        "cache_v": nrm(5, (DEPTH, DEC_BATCH, PAST_LEN, ATT_HEADS, V_DIM)),
        "state_conv": nrm(6, (DEPTH, DEC_BATCH, CONV_WIDTH - 1, LRU_WIDTH)),
        "state_h": nrm(7, (DEPTH, DEC_BATCH, LRU_WIDTH), 0.5),
        "g_mix": 1.0 + nrm(8, (DEPTH, D_MODEL), 0.01),
        "w_in": nrm(9, (DEPTH, D_MODEL, IN_COLS), D_MODEL ** -0.5),
        "lam_q1": nrm(10, (DEPTH, QK_DIM), 0.1),
        "lam_k1": nrm(11, (DEPTH, QK_DIM), 0.1),
        "lam_q2": nrm(12, (DEPTH, QK_DIM), 0.1),
        "lam_k2": nrm(13, (DEPTH, QK_DIM), 0.1),
        "g_subln": 1.0 + nrm(14, (DEPTH, V_DIM), 0.01),
        "conv_w": nrm(15, (DEPTH, CONV_WIDTH, LRU_WIDTH), CONV_WIDTH ** -0.5),
        "conv_b": nrm(16, (DEPTH, LRU_WIDTH), 0.01),
        "w_a": nrm(17, (DEPTH, LRU_BLOCKS, LRU_BLOCK_DIM, LRU_BLOCK_DIM), LRU_BLOCK_DIM ** -0.5),
        "b_a": nrm(18, (DEPTH, LRU_WIDTH), 0.01),
        "w_x": nrm(19, (DEPTH, LRU_BLOCKS, LRU_BLOCK_DIM, LRU_BLOCK_DIM), LRU_BLOCK_DIM ** -0.5),
        "b_x": nrm(20, (DEPTH, LRU_WIDTH), 0.01),
        "lru_lambda": jnp.log(sig / (1.0 - sig)),
        "w_out": nrm(21, (DEPTH, MIX_WIDTH, D_MODEL), MIX_WIDTH ** -0.5),
        "g_ffn": 1.0 + nrm(22, (DEPTH, D_MODEL), 0.01),
        "w_pq": nrm(23, (DEPTH, D_MODEL, PEER_HEADS * PEER_DKEY), D_MODEL ** -0.5),
        "sub_keys": nrm(24, (DEPTH, PEER_HEADS, 2, PEER_NKEYS, PEER_HALF), PEER_HALF ** -0.5),
        "expert_u": nrm(25, (DEPTH, PEER_EXPERTS, D_MODEL), D_MODEL ** -0.5),
        "expert_v": nrm(26, (DEPTH, PEER_EXPERTS, D_MODEL), PEER_HEADS ** -0.5),
        "g_ple": 1.0 + nrm(27, (DEPTH, D_MODEL), 0.01),
        "w_pg": nrm(28, (DEPTH, D_MODEL, D_MODEL), D_MODEL ** -0.5),
        "w_pe": nrm(29, (DEPTH, PLE_DIM, D_MODEL), PLE_DIM ** -0.5),
        "g_final": 1.0 + nrm(30, (D_MODEL,), 0.01),
    }


def reference(x_prompt, x_sample, p_prompt, p_sample, cache_k, cache_v, state_conv, state_h,
              g_mix, w_in, lam_q1, lam_k1, lam_q2, lam_k2, g_subln, conv_w, conv_b, w_a, b_a,
              w_x, b_x, lru_lambda, w_out, g_ffn, w_pq, sub_keys, expert_u, expert_v,
              g_ple, w_pg, w_pe, g_final):
    hp, hs = x_prompt, x_sample
    pos_p = jnp.arange(x_prompt.shape[1])
    past_len = cache_k.shape[2]
    pos_s = past_len + jnp.arange(x_sample.shape[1])
    kp_l, vp_l, cp_l, hp_l, ks_l, vs_l, cs_l, hs_l = [], [], [], [], [], [], [], []
    for l in range(DEPTH):
        lam_init = 0.8 - 0.6 * math.exp(-0.3 * l)
        w = dict(g_mix=g_mix[l], w_in=w_in[l], lam_q1=lam_q1[l], lam_k1=lam_k1[l],
                 lam_q2=lam_q2[l], lam_k2=lam_k2[l], g_subln=g_subln[l], conv_w=conv_w[l],
                 conv_b=conv_b[l], w_a=w_a[l], b_a=b_a[l], w_x=w_x[l], b_x=b_x[l],
                 lru_lambda=lru_lambda[l], w_out=w_out[l], g_ffn=g_ffn[l], w_pq=w_pq[l],
                 sub_keys=sub_keys[l], expert_u=expert_u[l], expert_v=expert_v[l],
                 g_ple=g_ple[l], w_pg=w_pg[l], w_pe=w_pe[l])
        zero_buf = jnp.zeros((hp.shape[0], CONV_WIDTH - 1, LRU_WIDTH), hp.dtype)
        zero_h = jnp.zeros((hp.shape[0], LRU_WIDTH), hp.dtype)
        hp, k_p, v_p, c_p, h_p = _layer(hp, p_prompt[l], pos_p, None, None, zero_buf, zero_h, lam_init, w)
        hs, k_s, v_s, c_s, h_s = _layer(hs, p_sample[l], pos_s, cache_k[l], cache_v[l],
                                        state_conv[l], state_h[l], lam_init, w)
        kp_l.append(k_p); vp_l.append(v_p); cp_l.append(c_p); hp_l.append(h_p)
        ks_l.append(k_s); vs_l.append(v_s); cs_l.append(c_s); hs_l.append(h_s)
    y_prompt = _rmsnorm(hp, g_final)
    y_sample = _rmsnorm(hs, g_final)
    return (y_prompt, y_sample,
            jnp.stack(kp_l), jnp.stack(vp_l), jnp.stack(cp_l), jnp.stack(hp_l),
            jnp.stack(ks_l), jnp.stack(vs_l), jnp.stack(cs_l), jnp.stack(hs_l))
```

```python
import functools
import math

import numpy as np
import jax
import jax.numpy as jnp
from jax import lax
from jax.experimental import pallas as pl
from jax.experimental.pallas import tpu as pltpu

F32 = jnp.float32
BF16 = jnp.bfloat16

CHUNK = 64
ATT_HEADS = 4
QK_DIM = 64
V_DIM = 128
ATT_WIDTH = ATT_HEADS * V_DIM
ROT_DIM = QK_DIM // 4
ROPE_THETA = 500000.0
LRU_WIDTH = 512
LRU_BLOCKS = 8
CONV_WIDTH = 4
LRU_C = 8.0
PEER_HEADS = 8
PEER_NKEYS = 128
PEER_HALF = 128
PEER_TOPK = 16
EPS = 1e-6
LANES = 128

VMEM_LIMIT_BYTES = 48 * 1024 * 1024

_CANDS = [(a, b) for a in range(PEER_TOPK) for b in range(PEER_TOPK) if (a + 1) * (b + 1) <= PEER_TOPK]
_NCAND_PAD = -(-len(_CANDS) // 8) * 8


def _params(*sem):
    return pltpu.CompilerParams(dimension_semantics=sem, vmem_limit_bytes=VMEM_LIMIT_BYTES)


def _rmsnorm(x, g):
    ms = jnp.mean(x * x, axis=-1, keepdims=True)
    return x * lax.rsqrt(ms + EPS) * g


def _gelu_tanh(x):
    return 0.5 * x * (1.0 + jnp.tanh(math.sqrt(2.0 / math.pi) * (x + 0.044715 * (x * x * x))))


def _sigmoid(x):
    return 1.0 / (1.0 + jnp.exp(-x))


def _rope_tables(pos0, t, reps):
    half = ROT_DIM // 2
    inv = ROPE_THETA ** (-np.arange(0, ROT_DIM, 2, dtype=np.float64) / ROT_DIM)
    ang = (pos0 + np.arange(t, dtype=np.float64))[:, None] * inv[None, :]
    cos64 = np.ones((t, QK_DIM))
    sa64 = np.zeros((t, QK_DIM))
    sb64 = np.zeros((t, QK_DIM))
    cos64[:, :half] = np.cos(ang)
    cos64[:, half:ROT_DIM] = np.cos(ang)
    sa64[:, :half] = -np.sin(ang)
    sb64[:, half:ROT_DIM] = np.sin(ang)
    out = []
    for tab in (cos64, sa64, sb64):
        tab = np.tile(np.concatenate([tab, tab], axis=1), (reps, 1))
        out.append(jnp.asarray(tab, dtype=F32))
    return out


def _inproj_kernel(x_ref, g_ref, w_ref, cos_ref, sa_ref, sb_ref,
                   qa_ref, qb_ref, kf_ref, kb_ref, vf_ref, vb_ref, xb_ref, gb_ref):
    a = _rmsnorm(x_ref[...], g_ref[...]).astype(BF16)
    z = jnp.dot(a, w_ref[...], preferred_element_type=F32)
    cos, sa, sb = cos_ref[...], sa_ref[...], sb_ref[...]
    half = ROT_DIM // 2
    lane = lax.broadcasted_iota(jnp.int32, cos.shape, 1)
    first = lane < QK_DIM
    for j in range(2 * ATT_HEADS):
        zj = z[:, j * LANES:(j + 1) * LANES]
        rot = zj * cos + pltpu.roll(zj, LANES - half, axis=1) * sa + pltpu.roll(zj, half, axis=1) * sb
        if j < ATT_HEADS:
            rs = rot * (QK_DIM ** -0.5)
            qa_ref[:, j * LANES:(j + 1) * LANES] = jnp.where(first, rs, 0.0).astype(BF16)
            qb_ref[:, j * LANES:(j + 1) * LANES] = jnp.where(first, 0.0, rs).astype(BF16)
        else:
            c0 = (j - ATT_HEADS) * LANES
            kf_ref[:, c0:c0 + LANES] = rot
            kb_ref[:, c0:c0 + LANES] = rot.astype(BF16)
    o2 = 2 * ATT_WIDTH
    v = z[:, o2:o2 + ATT_WIDTH]
    vf_ref[...] = v
    vb_ref[...] = v.astype(BF16)
    xb_ref[...] = z[:, o2 + ATT_WIDTH:o2 + ATT_WIDTH + LRU_WIDTH]
    gb_ref[...] = z[:, o2 + ATT_WIDTH + LRU_WIDTH:]


def _inproj(x, g_mix, w_in_bf, tabs, tm):
    n, d = x.shape
    ncols = w_in_bf.shape[1]
    nrep = tabs[0].shape[0] // tm
    row = lambda i: (i, 0)
    tab_spec = pl.BlockSpec((tm, LANES), lambda i: (i % nrep, 0))
    full = lambda shape: pl.BlockSpec(shape, lambda i: (0, 0))
    out_w = ATT_WIDTH
    outs = [jax.ShapeDtypeStruct((n, out_w), BF16), jax.ShapeDtypeStruct((n, out_w), BF16),
            jax.ShapeDtypeStruct((n, out_w), F32), jax.ShapeDtypeStruct((n, out_w), BF16),
            jax.ShapeDtypeStruct((n, out_w), F32), jax.ShapeDtypeStruct((n, out_w), BF16),
            jax.ShapeDtypeStruct((n, LRU_WIDTH), F32), jax.ShapeDtypeStruct((n, LRU_WIDTH), F32)]
    return pl.pallas_call(
        _inproj_kernel,
        grid=(n // tm,),
        in_specs=[pl.BlockSpec((tm, d), row), full((1, d)), full((d, ncols)), tab_spec, tab_spec, tab_spec],
        out_specs=[pl.BlockSpec((tm, out_w), row)] * 8,
        out_shape=outs,
        compiler_params=_params("parallel"),
        name="inproj",
    )(x, g_mix, w_in_bf, *tabs)


def _attn_kernel(qa_ref, qb_ref, k_ref, v_ref, lq1_ref, lk1_ref, lq2_ref, lk2_ref, gs_ref, o_ref,
                 m_ref, l_ref, acc_ref, *, tq, tk, nk, q_pos0, kv_len, lam_init):
    qi = pl.program_id(2)
    ki = pl.program_id(3)

    @pl.when(ki == 0)
    def _():
        m_ref[...] = jnp.full(m_ref.shape, -jnp.inf, F32)
        l_ref[...] = jnp.zeros(l_ref.shape, F32)
        acc_ref[...] = jnp.zeros(acc_ref.shape, F32)

    last_q_chunk = (q_pos0 + (qi + 1) * tq - 1) // CHUNK
    last_k = jnp.minimum(nk - 1, (last_q_chunk * CHUNK + CHUNK - 1) // tk)

    @pl.when(ki <= last_k)
    def _():
        q = jnp.concatenate([qa_ref[...], qb_ref[...]], axis=0)
        s = lax.dot_general(q, k_ref[...], (((1,), (1,)), ((), ())), preferred_element_type=F32)
        row = lax.broadcasted_iota(jnp.int32, s.shape, 0)
        col = lax.broadcasted_iota(jnp.int32, s.shape, 1)
        q_pos = q_pos0 + qi * tq + jnp.where(row >= tq, row - tq, row)
        k_pos = ki * tk + col
        visible = jnp.logical_and(k_pos // CHUNK <= q_pos // CHUNK, k_pos < kv_len)
        s = jnp.where(visible, s, -jnp.inf)
        m_prev = m_ref[...]
        m_new = jnp.maximum(m_prev, jnp.max(s, axis=1, keepdims=True))
        alpha = jnp.exp(m_prev - m_new)
        p = jnp.exp(s - m_new[:, :1])
        l_ref[...] = alpha * l_ref[...] + jnp.sum(p, axis=1, keepdims=True)
        acc_ref[...] = alpha * acc_ref[...] + jnp.dot(p.astype(BF16), v_ref[...], preferred_element_type=F32)
        m_ref[...] = m_new

    @pl.when(ki == nk - 1)
    def _():
        o12 = acc_ref[...] / l_ref[...]
        lam = (jnp.exp(jnp.sum(lq1_ref[...] * lk1_ref[...], axis=1, keepdims=True))
               - jnp.exp(jnp.sum(lq2_ref[...] * lk2_ref[...], axis=1, keepdims=True)) + lam_init)
        o = o12[:tq] - lam * o12[tq:]
        o_ref[...] = (_rmsnorm(o, gs_ref[...]) * (1.0 - lam_init)).astype(BF16)


def _attention(qa, qb, kb, vb, lam_vecs, g_subln, *, batch, t_q, t_kv, kv_len, q_pos0, tq, tk, lam_init):
    nq, nk = t_q // tq, t_kv // tk

    def kv_map(b, h, qi, ki):
        last_q_chunk = (q_pos0 + (qi + 1) * tq - 1) // CHUNK
        last_k = jnp.minimum(nk - 1, (last_q_chunk * CHUNK + CHUNK - 1) // tk)
        return (b * nk + jnp.minimum(ki, last_k), h)

    q_spec = pl.BlockSpec((tq, LANES), lambda b, h, qi, ki: (b * nq + qi, h))
    kv_spec = pl.BlockSpec((tk, LANES), kv_map)
    vec = lambda w: pl.BlockSpec((1, w), lambda b, h, qi, ki: (0, 0))
    kern = functools.partial(_attn_kernel, tq=tq, tk=tk, nk=nk, q_pos0=q_pos0, kv_len=kv_len, lam_init=lam_init)
    return pl.pallas_call(
        kern,
        grid=(batch, ATT_HEADS, nq, nk),
        in_specs=[q_spec, q_spec, kv_spec, kv_spec, vec(QK_DIM), vec(QK_DIM), vec(QK_DIM), vec(QK_DIM), vec(V_DIM)],
        out_specs=q_spec,
        out_shape=jax.ShapeDtypeStruct((batch * t_q, ATT_WIDTH), BF16),
        scratch_shapes=[pltpu.VMEM((2 * tq, LANES), F32), pltpu.VMEM((2 * tq, LANES), F32),
                        pltpu.VMEM((2 * tq, V_DIM), F32)],
        compiler_params=_params("parallel", "parallel", "parallel", "arbitrary"),
        name="diff_attention",
    )(qa, qb, kb, vb, *lam_vecs, g_subln)


def _lru_kernel(xb_ref, gb_ref, conv0_ref, h0_ref, cw_ref, cb_ref, wa_ref, ba_ref, wx_ref, bx_ref, lam_ref,
                r_ref, convn_ref, hl_ref, xpad_ref, hprev_ref, *, tt, nt):
    t = pl.program_id(1)
    pad = 8
    keep = CONV_WIDTH - 1

    @pl.when(t == 0)
    def _():
        xpad_ref[pad - keep:pad, :] = conv0_ref[...]
        hprev_ref[...] = h0_ref[...]

    @pl.when(t > 0)
    def _():
        xpad_ref[0:pad, :] = xpad_ref[tt:tt + pad, :]

    xpad_ref[pad:pad + tt, :] = xb_ref[...]
    xc = cb_ref[...]
    for j in range(CONV_WIDTH):
        xc = xc + xpad_ref[pad - keep + j:pad - keep + j + tt, :] * cw_ref[j:j + 1, :]
    xcb = xc.astype(BF16)
    r = _sigmoid(jnp.dot(xcb, wa_ref[...], preferred_element_type=F32) + ba_ref[...])
    i = _sigmoid(jnp.dot(xcb, wx_ref[...], preferred_element_type=F32) + bx_ref[...])
    nl = -lam_ref[...]
    softplus = jnp.maximum(nl, 0.0) + jnp.log1p(jnp.exp(-jnp.abs(nl)))
    log_a = (-LRU_C) * softplus * r
    a = jnp.exp(log_a)
    bv = jnp.sqrt(1.0 - jnp.exp(2.0 * log_a)) * (i * xc)
    rows = lax.broadcasted_iota(jnp.int32, a.shape, 0)
    d = 1
    while d < tt:
        keep_row = rows >= d
        a_sh = jnp.where(keep_row, pltpu.roll(a, d, axis=0), 1.0)
        b_sh = jnp.where(keep_row, pltpu.roll(bv, d, axis=0), 0.0)
        bv = a * b_sh + bv
        a = a * a_sh
        d *= 2
    h = bv + a * hprev_ref[...]
    hprev_ref[...] = h[tt - 1:tt, :]
    r_ref[...] = (h * _gelu_tanh(gb_ref[...])).astype(BF16)

    @pl.when(t == nt - 1)
    def _():
        convn_ref[...] = xpad_ref[pad + tt - keep:pad + tt, :]
        hl_ref[...] = h[tt - 1:tt, :]


def _rglru(xb, gb, conv0, h0, cw, cb, wa_bd, ba, wx_bd, bx, lru_lambda, *, batch, t, tt):
    nt = t // tt
    keep = CONV_WIDTH - 1
    w = LRU_WIDTH
    row = pl.BlockSpec((tt, w), lambda b, i: (b * nt + i, 0))
    full = lambda shape: pl.BlockSpec(shape, lambda b, i: (0,) * len(shape))
    per_b = lambda rows: pl.BlockSpec((None, rows, w), lambda b, i: (b, 0, 0))
    kern = functools.partial(_lru_kernel, tt=tt, nt=nt)
    return pl.pallas_call(
        kern,
        grid=(batch, nt),
        in_specs=[row, row, per_b(keep), per_b(1), full((CONV_WIDTH, w)), full((1, w)), full((w, w)), full((1, w)),
                  full((w, w)), full((1, w)), full((1, w))],
        out_specs=[row, per_b(keep), per_b(1)],
        out_shape=[jax.ShapeDtypeStruct((batch * t, w), BF16), jax.ShapeDtypeStruct((batch, keep, w), F32),
                   jax.ShapeDtypeStruct((batch, 1, w), F32)],
        scratch_shapes=[pltpu.VMEM((tt + 8, w), F32), pltpu.VMEM((1, w), F32)],
        compiler_params=_params("parallel", "arbitrary"),
        name="rglru",
    )(xb, gb, conv0, h0, cw, cb, wa_bd, ba, wx_bd, bx, lru_lambda)


def _outproj_kernel(o_ref, r_ref, x_ref, w_ref, g_ref, h_ref, ct_ref):
    mix = jnp.concatenate([o_ref[...], r_ref[...]], axis=1)
    h = x_ref[...] + jnp.dot(mix, w_ref[...], preferred_element_type=F32)
    h_ref[...] = h
    ct_ref[...] = _rmsnorm(h, g_ref[...]).T.astype(BF16)


def _outproj(o, r, x, w_out_bf, g_ffn, tm):
    n, d = x.shape
    row = lambda w: pl.BlockSpec((tm, w), lambda i: (i, 0))
    full = lambda shape: pl.BlockSpec(shape, lambda i: (0, 0))
    return pl.pallas_call(
        _outproj_kernel,
        grid=(n // tm,),
        in_specs=[row(ATT_WIDTH), row(LRU_WIDTH), row(d), full(w_out_bf.shape), full((1, d))],
        out_specs=[row(d), pl.BlockSpec((d, tm), lambda i: (0, i))],
        out_shape=[jax.ShapeDtypeStruct((n, d), F32), jax.ShapeDtypeStruct((d, n), BF16)],
        compiler_params=_params("parallel"),
        name="outproj",
    )(o, r, x, w_out_bf, g_ffn)


def _topk_rows(s, k):
    iota = lax.broadcasted_iota(jnp.int32, s.shape, 0)
    nkeys = s.shape[0]
    rank = jnp.full(s.shape, float(k), F32)
    vals = []
    for j in range(k):
        m = jnp.max(s, axis=0, keepdims=True)
        idx = jnp.min(jnp.where(s == m, iota, nkeys), axis=0, keepdims=True)
        hit = iota == idx
        rank = jnp.where(hit, float(j), rank)
        s = jnp.where(hit, -jnp.inf, s)
        vals.append(m)
    return vals, rank


def _route_kernel(ct_ref, wq_ref, keys_ref, fid_ref, l1_ref, e1_ref, r2_ref, e2_ref):
    qt = jnp.dot(wq_ref[...], ct_ref[...], preferred_element_type=F32)
    s1 = jnp.dot(keys_ref[0], qt[:PEER_HALF].astype(BF16), preferred_element_type=F32)
    s2 = jnp.dot(keys_ref[1], qt[PEER_HALF:].astype(BF16), preferred_element_type=F32)
    v1, rank1 = _topk_rows(s1, PEER_TOPK)
    v2, rank2 = _topk_rows(s2, PEER_TOPK)
    tm = s1.shape[1]
    cand_rows = [v1[a] + v2[b] for a, b in _CANDS]
    cand_rows += [jnp.full((1, tm), -jnp.inf, F32)] * (_NCAND_PAD - len(_CANDS))
    cv = jnp.concatenate(cand_rows, axis=0)
    flat_id = jnp.broadcast_to(fid_ref[:, 0:1], cv.shape)
    nflat = PEER_TOPK * PEER_TOPK
    l1 = jnp.zeros(rank1.shape, F32)
    z = jnp.zeros((1, tm), F32)
    top = v1[0] + v2[0]
    for _ in range(PEER_TOPK):
        m = jnp.max(cv, axis=0, keepdims=True)
        fid = jnp.min(jnp.where(cv == m, flat_id, nflat), axis=0, keepdims=True)
        cv = jnp.where(flat_id == fid, -jnp.inf, cv)
        k1_sel = lax.shift_right_logical(fid, int(math.log2(PEER_TOPK))).astype(F32)
        l1 = l1 + jnp.where(rank1 == k1_sel, 1.0, 0.0)
        z = z + jnp.exp(m - top)
    l1_ref[...] = l1
    e1_ref[...] = jnp.exp(s1 - v1[0]) / z
    r2_ref[...] = rank2
    e2_ref[...] = jnp.exp(s2 - v2[0])


def _peer_route(ct, wq_t_bf, keys_bf, tm):
    d, n = ct.shape
    out = jax.ShapeDtypeStruct((PEER_HEADS, PEER_NKEYS, n), F32)
    o_spec = pl.BlockSpec((None, PEER_NKEYS, tm), lambda i, h: (h, 0, i))
    fid = np.full((_NCAND_PAD, LANES), PEER_TOPK * PEER_TOPK, np.int32)
    fid[:len(_CANDS), :] = np.array([a * PEER_TOPK + b for a, b in _CANDS], np.int32)[:, None]
    return pl.pallas_call(
        _route_kernel,
        grid=(n // tm, PEER_HEADS),
        in_specs=[pl.BlockSpec((d, tm), lambda i, h: (0, i)),
                  pl.BlockSpec((2 * PEER_HALF, d), lambda i, h: (h, 0)),
                  pl.BlockSpec((None, 2, PEER_NKEYS, PEER_HALF), lambda i, h: (h, 0, 0, 0)),
                  pl.BlockSpec((_NCAND_PAD, LANES), lambda i, h: (0, 0))],
        out_specs=[o_spec] * 4,
        out_shape=[out] * 4,
        compiler_params=_params("parallel", "arbitrary"),
        name="peer_route",
    )(ct, wq_t_bf, keys_bf, jnp.asarray(fid))


def _peer_kernel(ct_ref, u_ref, vt_ref, l1_ref, e1_ref, r2_ref, e2_ref, h_ref, o_ref, acc_ref, wd_ref, *, eb, ne):
    e = pl.program_id(1)

    @pl.when(e == 0)
    def _():
        acc_ref[...] = jnp.zeros(acc_ref.shape, F32)

    ht = jnp.dot(u_ref[...], ct_ref[...], preferred_element_type=F32)
    for a in range(eb // PEER_NKEYS):
        i1 = e * (eb // PEER_NKEYS) + a
        g = None
        for h in range(PEER_HEADS):
            l1 = l1_ref[h, pl.ds(i1, 1), :]
            e1 = e1_ref[h, pl.ds(i1, 1), :]
            term = jnp.where(r2_ref[h] < l1, e2_ref[h] * e1, 0.0)
            g = term if g is None else g + term
        sl = slice(a * PEER_NKEYS, (a + 1) * PEER_NKEYS)
        wd_ref[sl, :] = (_gelu_tanh(ht[sl]) * g).astype(BF16)
    acc_ref[...] += jnp.dot(vt_ref[...], wd_ref[...], preferred_element_type=F32)

    @pl.when(e == ne - 1)
    def _():
        o_ref[...] = h_ref[...] + acc_ref[...].T


def _peer_dense(ct, u_bf, vt_bf, l1, e1, r2, e2, h1, tm, eb):
    d, n = ct.shape
    ne = u_bf.shape[0] // eb
    rt = pl.BlockSpec((PEER_HEADS, PEER_NKEYS, tm), lambda i, e: (0, 0, i))
    kern = functools.partial(_peer_kernel, eb=eb, ne=ne)
    return pl.pallas_call(
        kern,
        grid=(n // tm, ne),
        in_specs=[pl.BlockSpec((d, tm), lambda i, e: (0, i)),
                  pl.BlockSpec((eb, d), lambda i, e: (e, 0)),
                  pl.BlockSpec((d, eb), lambda i, e: (0, e)),
                  rt, rt, rt, rt,
                  pl.BlockSpec((tm, d), lambda i, e: (i, 0))],
        out_specs=pl.BlockSpec((tm, d), lambda i, e: (i, 0)),
        out_shape=jax.ShapeDtypeStruct((n, d), F32),
        scratch_shapes=[pltpu.VMEM((d, tm), F32), pltpu.VMEM((eb, tm), BF16)],
        compiler_params=_params("parallel", "arbitrary"),
        name="peer_dense",
    )(ct, u_bf, vt_bf, l1, e1, r2, e2, h1)


def _ple_kernel(h_ref, p_ref, gp_ref, wg_ref, we_ref, gf_ref, y_ref):
    h = h_ref[...]
    gate = _sigmoid(jnp.dot(_rmsnorm(h, gp_ref[...]).astype(BF16), wg_ref[...], preferred_element_type=F32))
    emb = jnp.dot(p_ref[...].astype(BF16), we_ref[...], preferred_element_type=F32)
    y_ref[...] = _rmsnorm(h + gate * emb, gf_ref[...])


def _ple_final(h2, p, g_ple, w_pg_bf, w_pe_bf, g_final, tm):
    n, d = h2.shape
    pd = p.shape[1]
    row = lambda w: pl.BlockSpec((tm, w), lambda i: (i, 0))
    full = lambda shape: pl.BlockSpec(shape, lambda i: (0, 0))
    return pl.pallas_call(
        _ple_kernel,
        grid=(n // tm,),
        in_specs=[row(d), row(pd), full((1, d)), full((d, d)), full((pd, d)), full((1, d))],
        out_specs=row(d),
        out_shape=jax.ShapeDtypeStruct((n, d), F32),
        compiler_params=_params("parallel"),
        name="ple_final",
    )(h2, p, g_ple, w_pg_bf, w_pe_bf, g_final)


def _block_diag(w):
    nb, bi, bo = w.shape
    eye = jnp.eye(nb, dtype=w.dtype)
    return (eye[:, None, :, None] * w[:, :, None, :]).reshape(nb * bi, nb * bo)


def _tile(n, pref):
    return pref if n % pref == 0 else n


def _stream(x, p, past_k, past_v, conv0, h0, w, lam_init, g_final):
    b, t, d = x.shape
    n = b * t
    x2 = x.reshape(n, d)
    q_pos0 = 0 if past_k is None else past_k.shape[1]

    tm = _tile(n, 512)
    t_tab = t if t % tm == 0 else n
    tabs = _rope_tables(q_pos0, t, t_tab // t)
    qa, qb, kf, kb, vf, vb, xb, gb = _inproj(x2, w["g_mix"], w["w_in"], tabs, tm)

    if past_k is None:
        kv_k, kv_v, t_kv, kv_len = kb, vb, t, t
        tq = _tile(t, 256)
        tk = _tile(t, 512)
    else:
        kv_len = q_pos0 + t
        t_kv = -(-kv_len // LANES) * LANES
        cat = lambda past, new: jnp.pad(
            jnp.concatenate([past.reshape(b, q_pos0, ATT_WIDTH).astype(BF16), new.reshape(b, t, ATT_WIDTH)], axis=1),
            ((0, 0), (0, t_kv - kv_len), (0, 0))).reshape(b * t_kv, ATT_WIDTH)
        kv_k, kv_v = cat(past_k, kb), cat(past_v, vb)
        tq, tk = t, t_kv
    o = _attention(qa, qb, kv_k, kv_v, w["lam_vecs"], w["g_subln"], batch=b, t_q=t, t_kv=t_kv, kv_len=kv_len,
                   q_pos0=q_pos0, tq=tq, tk=tk, lam_init=lam_init)

    r, conv_new, h_last = _rglru(xb, gb, conv0, h0[:, None, :], w["conv_w"], w["conv_b"], w["wa_bd"], w["b_a"],
                                 w["wx_bd"], w["b_x"], w["lru_lambda"], batch=b, t=t, tt=_tile(t, 512))

    h1, ct = _outproj(o, r, x2, w["w_out"], w["g_ffn"], tm)
    tr = _tile(n, 256)
    l1, e1, r2, e2 = _peer_route(ct, w["wq_t"], w["keys"], tr)
    h2 = _peer_dense(ct, w["u"], w["vt"], l1, e1, r2, e2, h1, _tile(n, 512), 512)
    y = _ple_final(h2, p.reshape(n, -1), w["g_ple"], w["w_pg"], w["w_pe"], g_final, tm)

    return (y.reshape(b, t, d),
            kf.reshape(1, b, t, ATT_HEADS, 2, QK_DIM), vf.reshape(1, b, t, ATT_HEADS, V_DIM),
            conv_new[None], h_last.reshape(1, b, LRU_WIDTH))


def kernel(x_prompt, x_sample, p_prompt, p_sample, cache_k, cache_v, state_conv, state_h, g_mix, w_in, lam_q1, lam_k1, lam_q2, lam_k2, g_subln, conv_w, conv_b, w_a, b_a, w_x, b_x, lru_lambda, w_out, g_ffn, w_pq, sub_keys, expert_u, expert_v, g_ple, w_pg, w_pe, g_final):
    depth = w_in.shape[0]
    assert depth == 1, "single-layer model"
    l = 0
    lam_init = 0.8 - 0.6 * math.exp(-0.3 * l)
    vec = lambda a: a[l][None, :]
    w = dict(
        g_mix=vec(g_mix), w_in=w_in[l].astype(BF16),
        lam_vecs=(vec(lam_q1), vec(lam_k1), vec(lam_q2), vec(lam_k2)), g_subln=vec(g_subln),
        conv_w=conv_w[l], conv_b=vec(conv_b), wa_bd=_block_diag(w_a[l]).astype(BF16), b_a=vec(b_a),
        wx_bd=_block_diag(w_x[l]).astype(BF16), b_x=vec(b_x), lru_lambda=vec(lru_lambda),
        w_out=w_out[l].astype(BF16), g_ffn=vec(g_ffn),
        wq_t=w_pq[l].T.astype(BF16), keys=sub_keys[l].astype(BF16),
        u=expert_u[l].astype(BF16), vt=expert_v[l].T.astype(BF16),
        g_ple=vec(g_ple), w_pg=w_pg[l].astype(BF16), w_pe=w_pe[l].astype(BF16),
    )
    gf = g_final[None, :]
    bp = x_prompt.shape[0]
    zero_buf = jnp.zeros((bp, CONV_WIDTH - 1, LRU_WIDTH), x_prompt.dtype)
    zero_h = jnp.zeros((bp, LRU_WIDTH), x_prompt.dtype)
    yp, kp, vp, cp, hp = _stream(x_prompt, p_prompt[l], None, None, zero_buf, zero_h, w, lam_init, gf)
    ys, ks, vs, cs, hs = _stream(x_sample, p_sample[l], cache_k[l], cache_v[l], state_conv[l], state_h[l], w,
                                 lam_init, gf)
    return (yp, ys, kp, vp, cp, hp, ks, vs, cs, hs)
```

```python
import functools
import math

import numpy as np
import jax
import jax.numpy as jnp
from jax import lax
from jax.experimental import pallas as pl
from jax.experimental.pallas import tpu as pltpu

F32 = jnp.float32
BF16 = jnp.bfloat16

CHUNK = 64
ATT_HEADS = 4
QK_DIM = 64
V_DIM = 128
ATT_WIDTH = ATT_HEADS * V_DIM
ROT_DIM = QK_DIM // 4
ROPE_THETA = 500000.0
LRU_WIDTH = 512
LRU_BLOCKS = 8
CONV_WIDTH = 4
LRU_C = 8.0
PEER_HEADS = 8
PEER_NKEYS = 128
PEER_HALF = 128
PEER_TOPK = 16
EPS = 1e-6
LANES = 128

VMEM_LIMIT_BYTES = 48 * 1024 * 1024

_CANDS = [(a, b) for a in range(PEER_TOPK) for b in range(PEER_TOPK) if (a + 1) * (b + 1) <= PEER_TOPK]
_NCAND_PAD = -(-len(_CANDS) // 8) * 8


def _params(*sem, flags=None):
    return pltpu.CompilerParams(dimension_semantics=sem, vmem_limit_bytes=VMEM_LIMIT_BYTES, flags=flags)


def _rmsnorm(x, g):
    ms = jnp.mean(x * x, axis=-1, keepdims=True)
    return x * lax.rsqrt(ms + EPS) * g


def _gelu_tanh(x):
    return 0.5 * x * (1.0 + jnp.tanh(math.sqrt(2.0 / math.pi) * (x + 0.044715 * (x * x * x))))


def _sigmoid(x):
    return 1.0 / (1.0 + jnp.exp(-x))


def _rope_tables(pos0, t, reps):
    half = ROT_DIM // 2
    inv = ROPE_THETA ** (-np.arange(0, ROT_DIM, 2, dtype=np.float64) / ROT_DIM)
    ang = (pos0 + np.arange(t, dtype=np.float64))[:, None] * inv[None, :]
    cos64 = np.ones((t, QK_DIM))
    sa64 = np.zeros((t, QK_DIM))
    sb64 = np.zeros((t, QK_DIM))
    cos64[:, :half] = np.cos(ang)
    cos64[:, half:ROT_DIM] = np.cos(ang)
    sa64[:, :half] = -np.sin(ang)
    sb64[:, half:ROT_DIM] = np.sin(ang)
    out = []
    for tab in (cos64, sa64, sb64):
        tab = np.tile(np.concatenate([tab, tab], axis=1), (reps, 1))
        out.append(jnp.asarray(tab, dtype=F32))
    return out


def _inproj_kernel(x_ref, g_ref, w_ref, cos_ref, sa_ref, sb_ref,
                   qa_ref, qb_ref, kf_ref, kb_ref, vf_ref, vb_ref, xb_ref, gb_ref):
    a = _rmsnorm(x_ref[...], g_ref[...]).astype(BF16)
    z = jnp.dot(a, w_ref[...], preferred_element_type=F32)
    cos, sa, sb = cos_ref[...], sa_ref[...], sb_ref[...]
    half = ROT_DIM // 2
    lane = lax.broadcasted_iota(jnp.int32, cos.shape, 1)
    first = lane < QK_DIM
    for j in range(2 * ATT_HEADS):
        zj = z[:, j * LANES:(j + 1) * LANES]
        rot = zj * cos + pltpu.roll(zj, LANES - half, axis=1) * sa + pltpu.roll(zj, half, axis=1) * sb
        if j < ATT_HEADS:
            rs = rot * (math.log2(math.e) * QK_DIM ** -0.5)
            qa_ref[:, j * LANES:(j + 1) * LANES] = jnp.where(first, rs, 0.0).astype(BF16)
            qb_ref[:, j * LANES:(j + 1) * LANES] = jnp.where(first, 0.0, rs).astype(BF16)
        else:
            c0 = (j - ATT_HEADS) * LANES
            kf_ref[:, c0:c0 + LANES] = rot
            kb_ref[:, c0:c0 + LANES] = rot.astype(BF16)
    o2 = 2 * ATT_WIDTH
    v = z[:, o2:o2 + ATT_WIDTH]
    vf_ref[...] = v
    vb_ref[...] = v.astype(BF16)
    xb_ref[...] = z[:, o2 + ATT_WIDTH:o2 + ATT_WIDTH + LRU_WIDTH]
    gb_ref[...] = z[:, o2 + ATT_WIDTH + LRU_WIDTH:]


def _inproj(x, g_mix, w_in_bf, tabs, tm):
    n, d = x.shape
    ncols = w_in_bf.shape[1]
    nrep = tabs[0].shape[0] // tm
    row = lambda i: (i, 0)
    tab_spec = pl.BlockSpec((tm, LANES), lambda i: (i % nrep, 0))
    full = lambda shape: pl.BlockSpec(shape, lambda i: (0, 0))
    out_w = ATT_WIDTH
    outs = [jax.ShapeDtypeStruct((n, out_w), BF16), jax.ShapeDtypeStruct((n, out_w), BF16),
            jax.ShapeDtypeStruct((n, out_w), F32), jax.ShapeDtypeStruct((n, out_w), BF16),
            jax.ShapeDtypeStruct((n, out_w), F32), jax.ShapeDtypeStruct((n, out_w), BF16),
            jax.ShapeDtypeStruct((n, LRU_WIDTH), F32), jax.ShapeDtypeStruct((n, LRU_WIDTH), F32)]
    return pl.pallas_call(
        _inproj_kernel,
        grid=(n // tm,),
        in_specs=[pl.BlockSpec((tm, d), row), full((1, d)), full((d, ncols)), tab_spec, tab_spec, tab_spec],
        out_specs=[pl.BlockSpec((tm, out_w), row)] * 8,
        out_shape=outs,
        compiler_params=_params("parallel"),
        name="inproj",
    )(x, g_mix, w_in_bf, *tabs)


def _attn_kernel(qa_ref, qb_ref, k_ref, v_ref, lq1_ref, lk1_ref, lq2_ref, lk2_ref, gs_ref, o_ref,
                 m_ref, l_ref, acc_ref, *, tq, tk, q_pos0, kv_len, lam_init):
    qi = pl.program_id(2)
    q = jnp.concatenate([qa_ref[...], qb_ref[...]], axis=0)
    m_ref[...] = jnp.full(m_ref.shape, -jnp.inf, F32)
    l_ref[...] = jnp.zeros(l_ref.shape, F32)
    acc_ref[...] = jnp.zeros(acc_ref.shape, F32)
    q_start = q_pos0 + qi * tq
    vis_all = jnp.minimum((q_start // CHUNK + 1) * CHUNK, kv_len)
    vis_any = jnp.minimum(((q_start + tq - 1) // CHUNK + 1) * CHUNK, kv_len)
    n_unmasked = vis_all // tk
    n_tiles = (vis_any + tk - 1) // tk

    def step(j, masked):
        off = pl.multiple_of(j * tk, tk)
        s = lax.dot_general(q, k_ref[pl.ds(off, tk), :], (((1,), (1,)), ((), ())),
                            preferred_element_type=F32)
        if masked:
            row = lax.broadcasted_iota(jnp.int32, s.shape, 0)
            col = lax.broadcasted_iota(jnp.int32, s.shape, 1)
            q_pos = q_start + jnp.where(row >= tq, row - tq, row)
            k_pos = off + col
            visible = jnp.logical_and(k_pos // CHUNK <= q_pos // CHUNK, k_pos < kv_len)
            s = jnp.where(visible, s, -jnp.inf)
        m_prev = m_ref[...]
        m_new = jnp.maximum(m_prev, jnp.max(s, axis=1, keepdims=True))
        alpha = jnp.exp2(m_prev - m_new)
        p = jnp.exp2(s - pltpu.repeat(m_new, tk // LANES, axis=1))
        l_ref[...] = alpha * l_ref[...] + jnp.sum(p, axis=1, keepdims=True)
        acc_ref[...] = alpha * acc_ref[...] + jnp.dot(p.astype(BF16), v_ref[pl.ds(off, tk), :],
                                                      preferred_element_type=F32)
        m_ref[...] = m_new

    def unmasked_body(j, carry):
        step(j, False)
        return carry

    def masked_body(j, carry):
        step(j, True)
        return carry

    lax.fori_loop(0, n_unmasked, unmasked_body, 0)
    lax.fori_loop(n_unmasked, n_tiles, masked_body, 0)

    o12 = acc_ref[...] / l_ref[...]
    lam = (jnp.exp(jnp.sum(lq1_ref[...] * lk1_ref[...], axis=1, keepdims=True))
           - jnp.exp(jnp.sum(lq2_ref[...] * lk2_ref[...], axis=1, keepdims=True)) + lam_init)
    o = o12[:tq] - lam * o12[tq:]
    o_ref[...] = (_rmsnorm(o, gs_ref[...]) * (1.0 - lam_init)).astype(BF16)


def _attention(qa, qb, kb, vb, lam_vecs, g_subln, *, batch, t_q, t_kv, kv_len, q_pos0, tq, tk, lam_init):
    nq = t_q // tq
    q_spec = pl.BlockSpec((tq, LANES), lambda b, h, qi: (b * nq + qi, h))
    kv_spec = pl.BlockSpec((t_kv, LANES), lambda b, h, qi: (b, h))
    vec = lambda w: pl.BlockSpec((1, w), lambda b, h, qi: (0, 0))
    kern = functools.partial(_attn_kernel, tq=tq, tk=tk, q_pos0=q_pos0, kv_len=kv_len, lam_init=lam_init)
    return pl.pallas_call(
        kern,
        grid=(batch, ATT_HEADS, nq),
        in_specs=[q_spec, q_spec, kv_spec, kv_spec, vec(QK_DIM), vec(QK_DIM), vec(QK_DIM), vec(QK_DIM), vec(V_DIM)],
        out_specs=q_spec,
        out_shape=jax.ShapeDtypeStruct((batch * t_q, ATT_WIDTH), BF16),
        scratch_shapes=[pltpu.VMEM((2 * tq, LANES), F32), pltpu.VMEM((2 * tq, LANES), F32),
                        pltpu.VMEM((2 * tq, V_DIM), F32)],
        compiler_params=_params("parallel", "parallel", "arbitrary"),
        name="diff_attention",
    )(qa, qb, kb, vb, *lam_vecs, g_subln)


def _lru_kernel(xb_ref, gb_ref, conv0_ref, h0_ref, cw_ref, cb_ref, wa_ref, ba_ref, wx_ref, bx_ref, lam_ref,
                r_ref, convn_ref, hl_ref, xpad_ref, hprev_ref, *, tt, nt):
    t = pl.program_id(1)
    pad = 8
    keep = CONV_WIDTH - 1

    @pl.when(t == 0)
    def _():
        xpad_ref[pad - keep:pad, :] = conv0_ref[...]
        hprev_ref[...] = h0_ref[...]

    @pl.when(t > 0)
    def _():
        xpad_ref[0:pad, :] = xpad_ref[tt:tt + pad, :]

    xpad_ref[pad:pad + tt, :] = xb_ref[...]
    xc = cb_ref[...]
    for j in range(CONV_WIDTH):
        xc = xc + xpad_ref[pad - keep + j:pad - keep + j + tt, :] * cw_ref[j:j + 1, :]
    xcb = xc.astype(BF16)
    r = _sigmoid(jnp.dot(xcb, wa_ref[...], preferred_element_type=F32) + ba_ref[...])
    i = _sigmoid(jnp.dot(xcb, wx_ref[...], preferred_element_type=F32) + bx_ref[...])
    nl = -lam_ref[...]
    softplus = jnp.maximum(nl, 0.0) + jnp.log1p(jnp.exp(-jnp.abs(nl)))
    log_a = (-LRU_C) * softplus * r
    a = jnp.exp(log_a)
    bv = jnp.sqrt(1.0 - jnp.exp(2.0 * log_a)) * (i * xc)
    rows = lax.broadcasted_iota(jnp.int32, a.shape, 0)
    d = 1
    while d < tt:
        keep_row = rows >= d
        a_sh = jnp.where(keep_row, pltpu.roll(a, d, axis=0), 1.0)
        b_sh = jnp.where(keep_row, pltpu.roll(bv, d, axis=0), 0.0)
        bv = a * b_sh + bv
        a = a * a_sh
        d *= 2
    h = bv + a * hprev_ref[...]
    hprev_ref[...] = h[tt - 1:tt, :]
    r_ref[...] = (h * _gelu_tanh(gb_ref[...])).astype(BF16)

    @pl.when(t == nt - 1)
    def _():
        convn_ref[...] = xpad_ref[pad + tt - keep:pad + tt, :]
        hl_ref[...] = h[tt - 1:tt, :]


def _rglru(xb, gb, conv0, h0, cw, cb, wa_bd, ba, wx_bd, bx, lru_lambda, *, batch, t, tt):
    nt = t // tt
    keep = CONV_WIDTH - 1
    w = LRU_WIDTH
    row = pl.BlockSpec((tt, w), lambda b, i: (b * nt + i, 0))
    full = lambda shape: pl.BlockSpec(shape, lambda b, i: (0,) * len(shape))
    per_b = lambda rows: pl.BlockSpec((None, rows, w), lambda b, i: (b, 0, 0))
    kern = functools.partial(_lru_kernel, tt=tt, nt=nt)
    return pl.pallas_call(
        kern,
        grid=(batch, nt),
        in_specs=[row, row, per_b(keep), per_b(1), full((CONV_WIDTH, w)), full((1, w)), full((w, w)), full((1, w)),
                  full((w, w)), full((1, w)), full((1, w))],
        out_specs=[row, per_b(keep), per_b(1)],
        out_shape=[jax.ShapeDtypeStruct((batch * t, w), BF16), jax.ShapeDtypeStruct((batch, keep, w), F32),
                   jax.ShapeDtypeStruct((batch, 1, w), F32)],
        scratch_shapes=[pltpu.VMEM((tt + 8, w), F32), pltpu.VMEM((1, w), F32)],
        compiler_params=_params("parallel", "arbitrary"),
        name="rglru",
    )(xb, gb, conv0, h0, cw, cb, wa_bd, ba, wx_bd, bx, lru_lambda)


def _outproj_kernel(o_ref, r_ref, x_ref, w_ref, g_ref, h_ref, ct_ref):
    mix = jnp.concatenate([o_ref[...], r_ref[...]], axis=1)
    h = x_ref[...] + jnp.dot(mix, w_ref[...], preferred_element_type=F32)
    h_ref[...] = h
    ct_ref[...] = _rmsnorm(h, g_ref[...]).T.astype(BF16)


def _outproj(o, r, x, w_out_bf, g_ffn, tm):
    n, d = x.shape
    row = lambda w: pl.BlockSpec((tm, w), lambda i: (i, 0))
    full = lambda shape: pl.BlockSpec(shape, lambda i: (0, 0))
    return pl.pallas_call(
        _outproj_kernel,
        grid=(n // tm,),
        in_specs=[row(ATT_WIDTH), row(LRU_WIDTH), row(d), full(w_out_bf.shape), full((1, d))],
        out_specs=[row(d), pl.BlockSpec((d, tm), lambda i: (0, i))],
        out_shape=[jax.ShapeDtypeStruct((n, d), F32), jax.ShapeDtypeStruct((d, n), BF16)],
        compiler_params=_params("parallel"),
        name="outproj",
    )(o, r, x, w_out_bf, g_ffn)


def _topk_rows(s, k):
    iota = lax.broadcasted_iota(jnp.int32, s.shape, 0)
    nkeys = s.shape[0]
    rank = jnp.full(s.shape, float(k), F32)
    vals = []
    for j in range(k):
        m = jnp.max(s, axis=0, keepdims=True)
        idx = jnp.min(jnp.where(s == m, iota, nkeys), axis=0, keepdims=True)
        hit = iota == idx
        rank = jnp.where(hit, float(j), rank)
        s = jnp.where(hit, -jnp.inf, s)
        vals.append(m)
    return vals, rank


def _route_kernel(ct_ref, wq_ref, keys_ref, fid_ref, l1_ref, e1_ref, r2_ref, e2_ref):
    qt = jnp.dot(wq_ref[...], ct_ref[...], preferred_element_type=F32)
    s1 = jnp.dot(keys_ref[0], qt[:PEER_HALF].astype(BF16), preferred_element_type=F32)
    s2 = jnp.dot(keys_ref[1], qt[PEER_HALF:].astype(BF16), preferred_element_type=F32)
    v1, rank1 = _topk_rows(s1, PEER_TOPK)
    v2, rank2 = _topk_rows(s2, PEER_TOPK)
    tm = s1.shape[1]
    cand_rows = [v1[a] + v2[b] for a, b in _CANDS]
    cand_rows += [jnp.full((1, tm), -jnp.inf, F32)] * (_NCAND_PAD - len(_CANDS))
    cv = jnp.concatenate(cand_rows, axis=0)
    flat_id = jnp.broadcast_to(fid_ref[:, 0:1], cv.shape)
    nflat = PEER_TOPK * PEER_TOPK
    l1 = jnp.zeros(rank1.shape, F32)
    z = jnp.zeros((1, tm), F32)
    top = v1[0] + v2[0]
    for _ in range(PEER_TOPK):
        m = jnp.max(cv, axis=0, keepdims=True)
        fid = jnp.min(jnp.where(cv == m, flat_id, nflat), axis=0, keepdims=True)
        cv = jnp.where(flat_id == fid, -jnp.inf, cv)
        k1_sel = lax.shift_right_logical(fid, int(math.log2(PEER_TOPK))).astype(F32)
        l1 = l1 + jnp.where(rank1 == k1_sel, 1.0, 0.0)
        z = z + jnp.exp(m - top)
    l1_ref[...] = l1
    e1_ref[...] = jnp.exp(s1 - v1[0]) / z
    r2_ref[...] = pltpu.bitcast(rank2.astype(BF16), jnp.uint32)
    e2_ref[...] = pltpu.bitcast(jnp.exp(s2 - v2[0]).astype(BF16), jnp.uint32)


def _peer_route(ct, wq_t_bf, keys_bf, tm):
    d, n = ct.shape
    rows = jax.ShapeDtypeStruct((PEER_HEADS, PEER_NKEYS, n), F32)
    planes = jax.ShapeDtypeStruct((PEER_HEADS, PEER_NKEYS // 2, n), jnp.uint32)
    row_spec = pl.BlockSpec((None, PEER_NKEYS, tm), lambda i, h: (h, 0, i))
    plane_spec = pl.BlockSpec((None, PEER_NKEYS // 2, tm), lambda i, h: (h, 0, i))
    fid = np.full((_NCAND_PAD, LANES), PEER_TOPK * PEER_TOPK, np.int32)
    fid[:len(_CANDS), :] = np.array([a * PEER_TOPK + b for a, b in _CANDS], np.int32)[:, None]
    return pl.pallas_call(
        _route_kernel,
        grid=(n // tm, PEER_HEADS),
        in_specs=[pl.BlockSpec((d, tm), lambda i, h: (0, i)),
                  pl.BlockSpec((2 * PEER_HALF, d), lambda i, h: (h, 0)),
                  pl.BlockSpec((None, 2, PEER_NKEYS, PEER_HALF), lambda i, h: (h, 0, 0, 0)),
                  pl.BlockSpec((_NCAND_PAD, LANES), lambda i, h: (0, 0))],
        out_specs=[row_spec, row_spec, plane_spec, plane_spec],
        out_shape=[rows, rows, planes, planes],
        compiler_params=_params("parallel", "arbitrary"),
        name="peer_route",
    )(ct, wq_t_bf, keys_bf, jnp.asarray(fid))


def _peer_kernel(ct_ref, u_ref, vt_ref, l1_ref, e1_ref, r2_ref, e2_ref, h_ref, o_ref, acc_ref, g_ref,
                 *, eb, sb, ne):
    e = pl.program_id(1)
    tm = ct_ref.shape[1]
    sub = 16
    nb = PEER_NKEYS // sub
    unpack = lambda words: pltpu.bitcast(words, BF16)
    n_i1 = eb // PEER_NKEYS

    @pl.when(e == 0)
    def _():
        acc_ref[...] = jnp.zeros(acc_ref.shape, F32)

    zero = jnp.zeros((sub, LANES), BF16)

    def gates(a):
        i1 = e * n_i1 + a
        l1_rows = [l1_ref[h, pl.ds(i1, 1), :] for h in range(PEER_HEADS)]
        e1_rows = [e1_ref[h, pl.ds(i1, 1), :] for h in range(PEER_HEADS)]
        for c in range(tm // LANES):
            cs = slice(c * LANES, (c + 1) * LANES)
            g = [None] * nb
            for h in range(PEER_HEADS):
                l1 = jnp.broadcast_to(l1_rows[h][:, cs], (sub, LANES)).astype(BF16)
                e1 = jnp.broadcast_to(e1_rows[h][:, cs], (sub, LANES)).astype(BF16)
                for b in range(nb):
                    bs = slice(b * sub // 2, (b + 1) * sub // 2)
                    term = jnp.where(unpack(r2_ref[h, bs, cs]) < l1, unpack(e2_ref[h, bs, cs]) * e1, zero)
                    g[b] = term if g[b] is None else g[b] + term
            for b in range(nb):
                r0 = (a * PEER_NKEYS + b * sub) // 2
                g_ref[r0:r0 + sub // 2, cs] = pltpu.bitcast(g[b], jnp.uint32)

    nsub = eb // sb
    per_sub = sb // PEER_NKEYS
    rows = lambda q: slice(q * sb, (q + 1) * sb)
    up = lambda q: jnp.dot(u_ref[rows(q), :], ct_ref[...], preferred_element_type=F32)
    ht = {0: up(0)}
    acc = None
    for q in range(nsub):
        if q + 1 < nsub:
            ht[q + 1] = up(q + 1)
        for a in range(q * per_sub, (q + 1) * per_sub):
            gates(a)
        wd = _gelu_tanh(ht.pop(q)).astype(BF16) * unpack(g_ref[q * sb // 2:(q + 1) * sb // 2, :])
        part = jnp.dot(vt_ref[:, rows(q)], wd, preferred_element_type=F32)
        acc = part if acc is None else acc + part
    acc_ref[...] += acc

    @pl.when(e == ne - 1)
    def _():
        o_ref[...] = h_ref[...] + acc_ref[...].T


def _peer_dense(ct, u_bf, vt_bf, l1, e1, r2, e2, h1, tm, eb):
    d, n = ct.shape
    ne = u_bf.shape[0] // eb
    rows = pl.BlockSpec((PEER_HEADS, PEER_NKEYS, tm), lambda i, e: (0, 0, i))
    planes = pl.BlockSpec((PEER_HEADS, PEER_NKEYS // 2, tm), lambda i, e: (0, 0, i))
    kern = functools.partial(_peer_kernel, eb=eb, sb=256, ne=ne)
    return pl.pallas_call(
        kern,
        grid=(n // tm, ne),
        in_specs=[pl.BlockSpec((d, tm), lambda i, e: (0, i)),
                  pl.BlockSpec((eb, d), lambda i, e: (e, 0)),
                  pl.BlockSpec((d, eb), lambda i, e: (0, e)),
                  rows, rows, planes, planes,
                  pl.BlockSpec((tm, d), lambda i, e: (i, 0))],
        out_specs=pl.BlockSpec((tm, d), lambda i, e: (i, 0)),
        out_shape=jax.ShapeDtypeStruct((n, d), F32),
        scratch_shapes=[pltpu.VMEM((d, tm), F32), pltpu.VMEM((eb // 2, tm), jnp.uint32)],
        compiler_params=_params("parallel", "arbitrary"),
        name="peer_dense",
    )(ct, u_bf, vt_bf, l1, e1, r2, e2, h1)


def _ple_kernel(h_ref, p_ref, gp_ref, wg_ref, we_ref, gf_ref, y_ref):
    h = h_ref[...]
    gate = _sigmoid(jnp.dot(_rmsnorm(h, gp_ref[...]).astype(BF16), wg_ref[...], preferred_element_type=F32))
    emb = jnp.dot(p_ref[...].astype(BF16), we_ref[...], preferred_element_type=F32)
    y_ref[...] = _rmsnorm(h + gate * emb, gf_ref[...])


def _ple_final(h2, p, g_ple, w_pg_bf, w_pe_bf, g_final, tm):
    n, d = h2.shape
    pd = p.shape[1]
    row = lambda w: pl.BlockSpec((tm, w), lambda i: (i, 0))
    full = lambda shape: pl.BlockSpec(shape, lambda i: (0, 0))
    return pl.pallas_call(
        _ple_kernel,
        grid=(n // tm,),
        in_specs=[row(d), row(pd), full((1, d)), full((d, d)), full((pd, d)), full((1, d))],
        out_specs=row(d),
        out_shape=jax.ShapeDtypeStruct((n, d), F32),
        compiler_params=_params("parallel"),
        name="ple_final",
    )(h2, p, g_ple, w_pg_bf, w_pe_bf, g_final)


def _block_diag(w):
    nb, bi, bo = w.shape
    eye = jnp.eye(nb, dtype=w.dtype)
    return (eye[:, None, :, None] * w[:, :, None, :]).reshape(nb * bi, nb * bo)


def _tile(n, pref):
    return pref if n % pref == 0 else n


def _stream(x, p, past_k, past_v, conv0, h0, w, lam_init, g_final):
    b, t, d = x.shape
    n = b * t
    x2 = x.reshape(n, d)
    q_pos0 = 0 if past_k is None else past_k.shape[1]

    tm = _tile(n, 512)
    t_tab = t if t % tm == 0 else n
    tabs = _rope_tables(q_pos0, t, t_tab // t)
    qa, qb, kf, kb, vf, vb, xb, gb = _inproj(x2, w["g_mix"], w["w_in"], tabs, tm)

    if past_k is None:
        kv_k, kv_v, t_kv, kv_len = kb, vb, t, t
        tq = _tile(t, 256)
        tk = _tile(t, 1024)
    else:
        kv_len = q_pos0 + t
        t_kv = -(-kv_len // LANES) * LANES
        cat = lambda past, new: jnp.pad(
            jnp.concatenate([past.reshape(b, q_pos0, ATT_WIDTH).astype(BF16), new.reshape(b, t, ATT_WIDTH)], axis=1),
            ((0, 0), (0, t_kv - kv_len), (0, 0))).reshape(b * t_kv, ATT_WIDTH)
        kv_k, kv_v = cat(past_k, kb), cat(past_v, vb)
        tq, tk = t, t_kv
    o = _attention(qa, qb, kv_k, kv_v, w["lam_vecs"], w["g_subln"], batch=b, t_q=t, t_kv=t_kv, kv_len=kv_len,
                   q_pos0=q_pos0, tq=tq, tk=tk, lam_init=lam_init)

    r, conv_new, h_last = _rglru(xb, gb, conv0, h0[:, None, :], w["conv_w"], w["conv_b"], w["wa_bd"], w["b_a"],
                                 w["wx_bd"], w["b_x"], w["lru_lambda"], batch=b, t=t, tt=_tile(t, 512))

    h1, ct = _outproj(o, r, x2, w["w_out"], w["g_ffn"], tm)
    tr = _tile(n, 256)
    l1, e1, r2, e2 = _peer_route(ct, w["wq_t"], w["keys"], tr)
    h2 = _peer_dense(ct, w["u"], w["vt"], l1, e1, r2, e2, h1, _tile(n, 512), 1024)
    y = _ple_final(h2, p.reshape(n, -1), w["g_ple"], w["w_pg"], w["w_pe"], g_final, tm)

    return (y.reshape(b, t, d),
            kf.reshape(1, b, t, ATT_HEADS, 2, QK_DIM), vf.reshape(1, b, t, ATT_HEADS, V_DIM),
            conv_new[None], h_last.reshape(1, b, LRU_WIDTH))


def kernel(x_prompt, x_sample, p_prompt, p_sample, cache_k, cache_v, state_conv, state_h, g_mix, w_in, lam_q1, lam_k1, lam_q2, lam_k2, g_subln, conv_w, conv_b, w_a, b_a, w_x, b_x, lru_lambda, w_out, g_ffn, w_pq, sub_keys, expert_u, expert_v, g_ple, w_pg, w_pe, g_final):
    depth = w_in.shape[0]
    assert depth == 1, "single-layer model"
    l = 0
    lam_init = 0.8 - 0.6 * math.exp(-0.3 * l)
    vec = lambda a: a[l][None, :]
    w = dict(
        g_mix=vec(g_mix), w_in=w_in[l].astype(BF16),
        lam_vecs=(vec(lam_q1), vec(lam_k1), vec(lam_q2), vec(lam_k2)), g_subln=vec(g_subln),
        conv_w=conv_w[l], conv_b=vec(conv_b), wa_bd=_block_diag(w_a[l]).astype(BF16), b_a=vec(b_a),
        wx_bd=_block_diag(w_x[l]).astype(BF16), b_x=vec(b_x), lru_lambda=vec(lru_lambda),
        w_out=w_out[l].astype(BF16), g_ffn=vec(g_ffn),
        wq_t=w_pq[l].T.astype(BF16), keys=sub_keys[l].astype(BF16),
        u=expert_u[l].astype(BF16), vt=expert_v[l].T.astype(BF16),
        g_ple=vec(g_ple), w_pg=w_pg[l].astype(BF16), w_pe=w_pe[l].astype(BF16),
    )
    gf = g_final[None, :]
    bp = x_prompt.shape[0]
    zero_buf = jnp.zeros((bp, CONV_WIDTH - 1, LRU_WIDTH), x_prompt.dtype)
    zero_h = jnp.zeros((bp, LRU_WIDTH), x_prompt.dtype)
    yp, kp, vp, cp, hp = _stream(x_prompt, p_prompt[l], None, None, zero_buf, zero_h, w, lam_init, gf)
    ys, ks, vs, cs, hs = _stream(x_sample, p_sample[l], cache_k[l], cache_v[l], state_conv[l], state_h[l], w,
                                 lam_init, gf)
    return (yp, ys, kp, vp, cp, hp, ks, vs, cs, hs)
```
